```python
import math
import jax, jax.numpy as jnp
from jax import lax
import numpy as np

D_MODEL = 2048
BATCH = 2
SEQ = 16384
DEPTH = 1

BLOCK = 128
EPS = 1e-6
SB_HEADS = 4
SB_HEAD_DIM = 128
SB_WIDTH = SB_HEADS * SB_HEAD_DIM
SW_HEADS = 16
SW_KV_HEADS = 2
SW_HEAD_DIM = 64
SW_WIDTH = SW_HEADS * SW_HEAD_DIM
SW_KV_WIDTH = SW_KV_HEADS * SW_HEAD_DIM
WINDOW = 128
OFF_Q_SB = 0
OFF_K_SB = OFF_Q_SB + SB_WIDTH
OFF_V_SB = OFF_K_SB + SB_WIDTH
OFF_Q_SW = OFF_V_SB + SB_WIDTH
OFF_K_SW = OFF_Q_SW + SW_WIDTH
OFF_V_SW = OFF_K_SW + SW_KV_WIDTH
OFF_G_SB = OFF_V_SW + SW_KV_WIDTH
OFF_G_SW = OFF_G_SB + D_MODEL
IN_WIDTH = OFF_G_SW + D_MODEL
N_GROUPS = 4
EXPERTS_PER_GROUP = 8
N_EXPERTS = N_GROUPS * EXPERTS_PER_GROUP
TOP_K_IN_GROUP = 2
D_EXPERT = 512
ROWS_PER_BLOCK = 512
NEG_INF = -1e30

kernel_name = "hybrid_stickbreak_swa_sink_hmoe"


def rms_norm(x, g):
    xf = x.astype(jnp.float32)
    y = xf * lax.rsqrt(jnp.mean(xf * xf, axis=-1, keepdims=True) + EPS)
    return (y * g.astype(jnp.float32)).astype(x.dtype)


def alibi_slopes(n_heads):
    h = jnp.arange(1, n_heads + 1, dtype=jnp.float32)
    return jnp.exp2(-8.0 * h / n_heads)


def stick_breaking_attention(q, k, v):
    b, s_len, h, d = q.shape
    nb = s_len // BLOCK
    scale = 1.0 / math.sqrt(d)
    qh = q.transpose(0, 2, 1, 3)
    kh = k.transpose(0, 2, 1, 3)
    vh = v.transpose(0, 2, 1, 3)
    outs = []
    for i in range(nb):
        n_keys = (i + 1) * BLOCK
        q_blk = qh[:, :, i * BLOCK:n_keys]
        k_pre = kh[:, :, :n_keys]
        v_pre = vh[:, :, :n_keys]
        z = jnp.einsum('bhqd,bhkd->bhqk', q_blk, k_pre).astype(jnp.float32) * scale
        q_pos = i * BLOCK + jnp.arange(BLOCK)
        mask = jnp.arange(n_keys)[None, :] < q_pos[:, None]
        log_beta = jax.nn.log_sigmoid(z)
        log_fail = jnp.where(mask, log_beta - z, 0.0)
        later = lax.cumsum(log_fail, axis=3, reverse=True) - log_fail
        a = jnp.where(mask, jnp.exp(log_beta + later), 0.0)
        outs.append(jnp.einsum('bhqk,bhkd->bhqd', a.astype(v_pre.dtype), v_pre))
    out = jnp.concatenate(outs, axis=2)
    return out.transpose(0, 2, 1, 3).reshape(b, s_len, h * d)


def sliding_window_attention(q, k, v, sinks):
    b, s_len, hq, d = q.shape
    kvh = k.shape[2]
    g = hq // kvh
    nb = s_len // BLOCK
    scale = 1.0 / math.sqrt(d)
    qb = q.reshape(b, nb, BLOCK, kvh, g, d)
    kb = k.reshape(b, nb, BLOCK, kvh, d)
    vb = v.reshape(b, nb, BLOCK, kvh, d)
    pad = ((0, 0), (1, 0), (0, 0), (0, 0), (0, 0))
    kc = jnp.concatenate([jnp.pad(kb[:, :-1], pad), kb], axis=2)
    vc = jnp.concatenate([jnp.pad(vb[:, :-1], pad), vb], axis=2)
    sc = jnp.einsum('bnqhgd,bnjhd->bnhgqj', qb, kc).astype(jnp.float32) * scale
    qi = jnp.arange(BLOCK)[:, None]
    kj = jnp.arange(2 * BLOCK)[None, :]
    dist = qi + BLOCK - kj
    key_abs = jnp.arange(nb)[:, None] * BLOCK - BLOCK + jnp.arange(2 * BLOCK)[None, :]
    mask = ((dist >= 0) & (dist < WINDOW))[None] & (key_abs >= 0)[:, None, :]
    slopes = alibi_slopes(hq).reshape(kvh, g)
    sc = sc - slopes[:, :, None, None] * dist.astype(jnp.float32)
    sc = jnp.where(mask[None, :, None, None], sc, NEG_INF)
    sink = sinks.astype(jnp.float32).reshape(kvh, g)[None, None, :, :, None, None]
    m = jnp.maximum(jnp.max(sc, axis=-1, keepdims=True), sink)
    p = jnp.exp(sc - m)
    p = p / (jnp.sum(p, axis=-1, keepdims=True) + jnp.exp(sink - m))
    o = jnp.einsum('bnhgqj,bnjhd->bnqhgd', p.astype(vc.dtype), vc)
    return o.reshape(b, s_len, hq * d)


def hierarchical_moe(h, w_router_group, b_router_group, w_router_expert, b_router_expert,
                     w_gate, w_up, w_down):
    b, s_len, d = h.shape
    t = b * s_len
    ht = h.reshape(t, d)
    group_prob = jax.nn.softmax((ht @ w_router_group).astype(jnp.float32) + b_router_group.astype(jnp.float32), axis=-1)
    p_grp, g_idx = lax.top_k(group_prob, 1)
    exp_logits = (ht @ w_router_expert).astype(jnp.float32) + b_router_expert.astype(jnp.float32)
    exp_logits = exp_logits.reshape(t, N_GROUPS, EXPERTS_PER_GROUP)
    sel_logits = jnp.take_along_axis(exp_logits, g_idx[:, :, None], axis=1)[:, 0]
    p_in = jax.nn.softmax(sel_logits, axis=-1)
    top_p, top_i = lax.top_k(p_in, TOP_K_IN_GROUP)
    top_p = top_p / jnp.sum(top_p, axis=-1, keepdims=True)
    expert_id = g_idx * EXPERTS_PER_GROUP + top_i
    gate = p_grp * top_p

    n_assign = t * TOP_K_IN_GROUP
    flat_e = expert_id.reshape(-1)
    flat_w = gate.reshape(-1)
    flat_tok = jnp.arange(n_assign, dtype=jnp.int32) // TOP_K_IN_GROUP
    order = jnp.argsort(flat_e, stable=True)
    e_sorted = flat_e[order]
    tok_sorted = flat_tok[order]
    w_sorted = flat_w[order]
    counts = jnp.bincount(flat_e, length=N_EXPERTS)
    padded = (counts + ROWS_PER_BLOCK - 1) // ROWS_PER_BLOCK * ROWS_PER_BLOCK
    start = jnp.cumsum(counts) - counts
    pend = jnp.cumsum(padded)
    pstart = pend - padded
    dest = pstart[e_sorted] + (jnp.arange(n_assign) - start[e_sorted])
    n_blocks = -(-n_assign // ROWS_PER_BLOCK) + N_EXPERTS
    n_rows = n_blocks * ROWS_PER_BLOCK
    row_tok = jnp.zeros((n_rows,), jnp.int32).at[dest].set(tok_sorted)
    row_w = jnp.zeros((n_rows,), jnp.float32).at[dest].set(w_sorted)
    block_e = jnp.clip(jnp.searchsorted(pend, jnp.arange(n_blocks) * ROWS_PER_BLOCK, side='right'),
                       0, N_EXPERTS - 1)
    x_rows = ht[row_tok].reshape(n_blocks, ROWS_PER_BLOCK, d)

    def expert_block(args):
        xb, e, wb = args
        a = xb @ w_gate[e]
        u = xb @ w_up[e]
        return ((jax.nn.silu(a) * u) @ w_down[e]) * wb[:, None].astype(xb.dtype)

    y_rows = lax.map(expert_block, (x_rows, block_e, row_w.reshape(n_blocks, ROWS_PER_BLOCK)))
    y = jax.ops.segment_sum(y_rows.reshape(n_rows, d), row_tok, num_segments=t)
    return y.reshape(b, s_len, d)


def setup_inputs(seed: int = 0) -> dict:
    key = jax.random.key(seed)
    ks = jax.random.split(key, 17)
    f32 = jnp.float32
    L, D = DEPTH, D_MODEL

    def nrm(k, shape, fan_in):
        return jax.random.normal(k, shape, f32) * (fan_in ** -0.5)

    return {
        "x": jax.random.normal(ks[0], (BATCH, SEQ, D), f32),
        "norm_mix": 1.0 + 0.02 * jax.random.normal(ks[1], (L, D), f32),
        "w_in": nrm(ks[2], (L, D, IN_WIDTH), D),
        "sinks": 0.5 * jax.random.normal(ks[3], (L, SW_HEADS), f32),
        "w_up_sb": nrm(ks[4], (L, SB_WIDTH, D), SB_WIDTH),
        "w_up_sw": nrm(ks[5], (L, SW_WIDTH, D), SW_WIDTH),
        "w_out": nrm(ks[6], (L, D, D), D),
        "norm_ffn": 1.0 + 0.02 * jax.random.normal(ks[7], (L, D), f32),
        "w_router_group": nrm(ks[8], (L, D, N_GROUPS), D),
        "b_router_group": 0.01 * jax.random.normal(ks[9], (L, N_GROUPS), f32),
        "w_router_expert": nrm(ks[10], (L, D, N_EXPERTS), D),
        "b_router_expert": 0.01 * jax.random.normal(ks[11], (L, N_EXPERTS), f32),
        "w_gate": nrm(ks[12], (L, N_EXPERTS, D, D_EXPERT), D),
        "w_up": nrm(ks[13], (L, N_EXPERTS, D, D_EXPERT), D),
        "w_down": nrm(ks[14], (L, N_EXPERTS, D_EXPERT, D), D_EXPERT),
        "norm_final": 1.0 + 0.02 * jax.random.normal(ks[15], (D,), f32),
    }


def reference(x, norm_mix, w_in, sinks, w_up_sb, w_up_sw, w_out, norm_ffn,
              w_router_group, b_router_group, w_router_expert, b_router_expert,
              w_gate, w_up, w_down, norm_final):
    b, s_len, _ = x.shape
    for layer in range(DEPTH):
        h = rms_norm(x, norm_mix[layer])
        proj = h @ w_in[layer]
        q_sb = proj[..., OFF_Q_SB:OFF_K_SB].reshape(b, s_len, SB_HEADS, SB_HEAD_DIM)
        k_sb = proj[..., OFF_K_SB:OFF_V_SB].reshape(b, s_len, SB_HEADS, SB_HEAD_DIM)
        v_sb = proj[..., OFF_V_SB:OFF_Q_SW].reshape(b, s_len, SB_HEADS, SB_HEAD_DIM)
        q_sw = proj[..., OFF_Q_SW:OFF_K_SW].reshape(b, s_len, SW_HEADS, SW_HEAD_DIM)
        k_sw = proj[..., OFF_K_SW:OFF_V_SW].reshape(b, s_len, SW_KV_HEADS, SW_HEAD_DIM)
        v_sw = proj[..., OFF_V_SW:OFF_G_SB].reshape(b, s_len, SW_KV_HEADS, SW_HEAD_DIM)
        gate_sb = jax.nn.sigmoid(proj[..., OFF_G_SB:OFF_G_SW])
        gate_sw = jax.nn.sigmoid(proj[..., OFF_G_SW:IN_WIDTH])
        y_sb = stick_breaking_attention(q_sb, k_sb, v_sb) @ w_up_sb[layer]
        y_sw = sliding_window_attention(q_sw, k_sw, v_sw, sinks[layer]) @ w_up_sw[layer]
        x = x + (gate_sb * y_sb + gate_sw * y_sw) @ w_out[layer]
        h2 = rms_norm(x, norm_ffn[layer])
        x = x + hierarchical_moe(h2, w_router_group[layer], b_router_group[layer],
                                 w_router_expert[layer], b_router_expert[layer],
                                 w_gate[layer], w_up[layer], w_down[layer])
    return rms_norm(x, norm_final)
```

```python
import functools
import math

import jax
import jax.numpy as jnp
from jax import lax
from jax.experimental import pallas as pl
from jax.experimental.pallas import tpu as pltpu

F32 = jnp.float32
BF16 = jnp.bfloat16
U32 = jnp.uint32
I32 = jnp.int32

EPS = 1e-6
BLOCK = 128
SB_HEADS = 4
SB_HEAD_DIM = 128
SB_WIDTH = SB_HEADS * SB_HEAD_DIM
SW_HEADS = 16
SW_KV_HEADS = 2
SW_HEAD_DIM = 64
SW_WIDTH = SW_HEADS * SW_HEAD_DIM
SW_KV_WIDTH = SW_KV_HEADS * SW_HEAD_DIM
WINDOW = 128
N_GROUPS = 4
EXPERTS_PER_GROUP = 8
N_EXPERTS = N_GROUPS * EXPERTS_PER_GROUP
TOP_K = 2
ROWS_PER_BLOCK = 512
NEG_INF = -1e30
LANES = 128

COL_G_SB = 0
COL_G_SW = 2048
COL_Q_SW = 4096
COL_Q_SB = 5120
COL_K_SB = 5632
COL_V_SB = 6144
COL_K_SW = 6656
COL_V_SW = 6784

SB_EXIT_LOG = -110.0
SB_WINDOW_BLOCKS = 3

VMEM_LIMIT = 48 * 1024 * 1024


def _sigmoid(v):
    return 1.0 / (1.0 + jnp.exp(-v))


def _pack_bf16_pair(lo, hi):
    lo_bits = lax.bitcast_convert_type(lo.astype(BF16).astype(F32), U32)
    hi_bits = lax.bitcast_convert_type(hi.astype(BF16).astype(F32), U32)
    return (lo_bits >> 16) | hi_bits


def _unpack_bf16_pair(w):
    lo = lax.bitcast_convert_type(w << 16, F32)
    hi = lax.bitcast_convert_type(w & jnp.uint32(0xFFFF0000), F32)
    return lo, hi


def _inproj_body(x_ref, g_ref, w_ref, o_ref, h_scr, *, tn, n_gate_cols):
    j = pl.program_id(1)

    @pl.when(j == 0)
    def _():
        xf = x_ref[...]
        ms = jnp.mean(xf * xf, axis=-1, keepdims=True)
        h_scr[...] = (xf * lax.rsqrt(ms + EPS) * g_ref[...]).astype(BF16)

    acc = jnp.dot(h_scr[...], w_ref[...], preferred_element_type=F32)
    n_full, rem = divmod(n_gate_cols, tn)

    @pl.when(j < n_full)
    def _():
        o_ref[...] = _sigmoid(acc).astype(o_ref.dtype)

    if rem:
        @pl.when(j == n_full)
        def _():
            o_ref[:, :rem] = _sigmoid(acc[:, :rem]).astype(o_ref.dtype)
            o_ref[:, rem:] = acc[:, rem:].astype(o_ref.dtype)

    @pl.when(j >= n_full + (1 if rem else 0))
    def _():
        o_ref[...] = acc.astype(o_ref.dtype)


def _inproj(x2d, gain, w_perm, *, tm, tn, n_gate_cols):
    t, d = x2d.shape
    n = w_perm.shape[1]
    return pl.pallas_call(
        functools.partial(_inproj_body, tn=tn, n_gate_cols=n_gate_cols),
        out_shape=jax.ShapeDtypeStruct((t, n), BF16),
        grid=(t // tm, n // tn),
        in_specs=[
            pl.BlockSpec((tm, d), lambda i, j: (i, 0)),
            pl.BlockSpec((1, d), lambda i, j: (0, 0)),
            pl.BlockSpec((d, tn), lambda i, j: (0, j)),
        ],
        out_specs=pl.BlockSpec((tm, tn), lambda i, j: (i, j)),
        scratch_shapes=[pltpu.VMEM((tm, d), BF16)],
        compiler_params=pltpu.CompilerParams(
            dimension_semantics=("parallel", "arbitrary"), vmem_limit_bytes=VMEM_LIMIT),
        name="inproj",
    )(x2d, gain, w_perm)


def _sb_terms(z):
    e = jnp.log(1.0 + jnp.exp(-jnp.abs(z)))
    return jnp.minimum(z, 0.0) - e, -jnp.maximum(z, 0.0) - e


def _split_bf16(v):
    hi = v.astype(BF16)
    lo = (v - hi.astype(F32)).astype(BF16)
    return hi, lo


def _sb_body(q_ref, k0_ref, k1_ref, k2_ref, v0_ref, v1_ref, v2_ref, tri_ref, proj_hbm,
             o_ref, acc_scr, run_scr, kbuf, vbuf, sem, *, seq_len):
    b = pl.program_id(0)
    i = pl.program_id(1)
    scale = 1.0 / math.sqrt(SB_HEAD_DIM)
    wkeys = SB_WINDOW_BLOCKS * BLOCK
    ws = jnp.maximum(i - (SB_WINDOW_BLOCKS - 1), 0)
    qpos = i * BLOCK + lax.broadcasted_iota(I32, (BLOCK, wkeys), 0)
    kpos = ws * BLOCK + lax.broadcasted_iota(I32, (BLOCK, wkeys), 1)
    mask = kpos < qpos
    tri_hi = tri_ref[...]

    run_max = None
    for h in range(SB_HEADS):
        hs = slice(h * SB_HEAD_DIM, (h + 1) * SB_HEAD_DIM)
        q = q_ref[:, hs]
        k = jnp.concatenate([k0_ref[:, hs], k1_ref[:, hs], k2_ref[:, hs]], axis=0)
        v = jnp.concatenate([v0_ref[:, hs], v1_ref[:, hs], v2_ref[:, hs]], axis=0)
        z = lax.dot_general(q, k, (((1,), (1,)), ((), ())), preferred_element_type=F32) * scale
        log_beta, log_fail = _sb_terms(z)
        lf = jnp.where(mask, log_fail, 0.0)
        hi, lo = _split_bf16(lf)
        later = jnp.dot(jnp.concatenate([hi, lo], axis=1), tri_hi, preferred_element_type=F32)
        a = jnp.where(mask, jnp.exp(log_beta + later), 0.0)
        acc_scr[:, hs] = jnp.dot(a.astype(BF16), v, preferred_element_type=F32)
        run = jnp.sum(lf, axis=1, keepdims=True)
        run_scr[:, h:h + 1] = run
        m = jnp.max(run)
        run_max = m if run_max is None else jnp.maximum(run_max, m)

    tri128 = tri_ref[0:BLOCK, 0:BLOCK]

    def tail_cond(carry):
        kb, rmax = carry
        return jnp.logical_and(kb >= 0, rmax > SB_EXIT_LOG)

    def tail_body(carry):
        kb, _ = carry
        row0 = pl.multiple_of(b * seq_len + kb * BLOCK, BLOCK)
        ck = pltpu.make_async_copy(
            proj_hbm.at[pl.ds(row0, BLOCK), pl.ds(COL_K_SB, SB_WIDTH)], kbuf, sem.at[0])
        cv = pltpu.make_async_copy(
            proj_hbm.at[pl.ds(row0, BLOCK), pl.ds(COL_V_SB, SB_WIDTH)], vbuf, sem.at[1])
        ck.start()
        cv.start()
        ck.wait()
        cv.wait()
        new_max = None
        for h in range(SB_HEADS):
            hs = slice(h * SB_HEAD_DIM, (h + 1) * SB_HEAD_DIM)
            z = lax.dot_general(q_ref[:, hs], kbuf[:, hs], (((1,), (1,)), ((), ())),
                                preferred_element_type=F32) * scale
            log_beta, lf = _sb_terms(z)
            hi, lo = _split_bf16(lf)
            run = run_scr[:, h:h + 1]
            later = (jnp.dot(hi, tri128, preferred_element_type=F32)
                     + jnp.dot(lo, tri128, preferred_element_type=F32)) + run
            a = jnp.exp(log_beta + later)
            acc_scr[:, hs] += jnp.dot(a.astype(BF16), vbuf[:, hs], preferred_element_type=F32)
            run = run + jnp.sum(lf, axis=1, keepdims=True)
            run_scr[:, h:h + 1] = run
            m = jnp.max(run)
            new_max = m if new_max is None else jnp.maximum(new_max, m)
        return kb - 1, new_max

    lax.while_loop(tail_cond, tail_body, (ws - 1, run_max))
    o_ref[...] = acc_scr[...].astype(o_ref.dtype)


def _sb_attention(proj, tri, *, batch, seq_len):
    nb = seq_len // BLOCK
    assert nb >= SB_WINDOW_BLOCKS
    t = proj.shape[0]
    wcol = SB_WIDTH

    def qmap(b, i):
        return (b * nb + i, COL_Q_SB // wcol)

    def kvmap(col, j):
        def f(b, i):
            return (b * nb + jnp.maximum(i - (SB_WINDOW_BLOCKS - 1), 0) + j, col // wcol)
        return f

    blk = (BLOCK, wcol)
    in_specs = [pl.BlockSpec(blk, qmap)]
    in_specs += [pl.BlockSpec(blk, kvmap(COL_K_SB, j)) for j in range(SB_WINDOW_BLOCKS)]
    in_specs += [pl.BlockSpec(blk, kvmap(COL_V_SB, j)) for j in range(SB_WINDOW_BLOCKS)]
    in_specs += [pl.BlockSpec(tri.shape, lambda b, i: (0, 0)),
                 pl.BlockSpec(memory_space=pl.ANY)]
    return pl.pallas_call(
        functools.partial(_sb_body, seq_len=seq_len),
        out_shape=jax.ShapeDtypeStruct((t, SB_WIDTH), BF16),
        grid=(batch, nb),
        in_specs=in_specs,
        out_specs=pl.BlockSpec((BLOCK, SB_WIDTH), lambda b, i: (b * nb + i, 0)),
        scratch_shapes=[
            pltpu.VMEM((BLOCK, SB_WIDTH), F32),
            pltpu.VMEM((BLOCK, LANES), F32),
            pltpu.VMEM((BLOCK, SB_WIDTH), BF16),
            pltpu.VMEM((BLOCK, SB_WIDTH), BF16),
            pltpu.SemaphoreType.DMA((2,)),
        ],
        compiler_params=pltpu.CompilerParams(
            dimension_semantics=("parallel", "arbitrary"), vmem_limit_bytes=VMEM_LIMIT),
        name="sb_attention",
    )(proj, proj, proj, proj, proj, proj, proj, tri, proj)


def _sw_body(sink_ref, q_ref, kp_ref, kc_ref, vp_ref, vc_ref, o_ref):
    i = pl.program_id(1)
    scale = 1.0 / math.sqrt(SW_HEAD_DIM)
    group = SW_HEADS // SW_KV_HEADS
    qi = lax.broadcasted_iota(I32, (BLOCK, 2 * BLOCK), 0)
    kj = lax.broadcasted_iota(I32, (BLOCK, 2 * BLOCK), 1)
    dist = qi + BLOCK - kj
    key_abs = (i - 1) * BLOCK + kj
    mask = (dist >= 0) & (dist < WINDOW) & (key_abs >= 0)
    distf = dist.astype(F32)
    for c in range(SW_KV_HEADS):
        cs = slice(c * SW_HEAD_DIM, (c + 1) * SW_HEAD_DIM)
        k = jnp.concatenate([kp_ref[:, cs], kc_ref[:, cs]], axis=0)
        v = jnp.concatenate([vp_ref[:, cs], vc_ref[:, cs]], axis=0)
        for g in range(group):
            h = c * group + g
            slope = 2.0 ** (-8.0 * (h + 1) / SW_HEADS)
            hs = slice(h * SW_HEAD_DIM, (h + 1) * SW_HEAD_DIM)
            sc = lax.dot_general(q_ref[:, hs], k, (((1,), (1,)), ((), ())),
                                 preferred_element_type=F32) * scale
            sc = jnp.where(mask, sc - slope * distf, NEG_INF)
            sink = sink_ref[h]
            m = jnp.maximum(jnp.max(sc, axis=-1, keepdims=True), sink)
            p = jnp.exp(sc - m)
            denom = jnp.sum(p, axis=-1, keepdims=True) + jnp.exp(sink - m)
            o = jnp.dot(p.astype(BF16), v, preferred_element_type=F32) / denom
            o_ref[:, hs] = o.astype(o_ref.dtype)


def _sw_attention(proj, sinks, *, batch, seq_len):
    nb = seq_len // BLOCK
    t = proj.shape[0]

    def prev(col):
        return lambda b, i, s: (b * nb + jnp.maximum(i - 1, 0), col // SW_KV_WIDTH)

    def cur(col):
        return lambda b, i, s: (b * nb + i, col // SW_KV_WIDTH)

    kvblk = (BLOCK, SW_KV_WIDTH)
    grid_spec = pltpu.PrefetchScalarGridSpec(
        num_scalar_prefetch=1,
        grid=(batch, nb),
        in_specs=[
            pl.BlockSpec((BLOCK, SW_WIDTH), lambda b, i, s: (b * nb + i, COL_Q_SW // SW_WIDTH)),
            pl.BlockSpec(kvblk, prev(COL_K_SW)),
            pl.BlockSpec(kvblk, cur(COL_K_SW)),
            pl.BlockSpec(kvblk, prev(COL_V_SW)),
            pl.BlockSpec(kvblk, cur(COL_V_SW)),
        ],
        out_specs=pl.BlockSpec((BLOCK, SW_WIDTH), lambda b, i, s: (b * nb + i, 0)),
    )
    return pl.pallas_call(
        _sw_body,
        out_shape=jax.ShapeDtypeStruct((t, SW_WIDTH), BF16),
        grid_spec=grid_spec,
        compiler_params=pltpu.CompilerParams(
            dimension_semantics=("parallel", "parallel"), vmem_limit_bytes=VMEM_LIMIT),
        name="sw_attention",
    )(sinks, proj, proj, proj, proj, proj)


def _first_index_of_max(vals, lane, valid):
    masked = jnp.where(valid, vals, jnp.finfo(F32).min)
    m = jnp.max(masked, axis=-1, keepdims=True)
    lane_f = lane.astype(F32)
    idx = jnp.min(jnp.where(valid & (masked == m), lane_f, float(LANES)), axis=-1, keepdims=True)
    return m, idx.astype(I32)


def _mix_body(x_ref, osb_ref, osw_ref, gsb_ref, gsw_ref, wsb_ref, wsw_ref, wout_ref,
              nrm_ref, wr_hi_ref, wr_lo_ref, br_ref,
              x1_ref, h2_ref, ri_ref, rw_ref):
    y_sb = jnp.dot(osb_ref[...], wsb_ref[...], preferred_element_type=F32)
    y_sw = jnp.dot(osw_ref[...], wsw_ref[...], preferred_element_type=F32)
    mixed = gsb_ref[...].astype(F32) * y_sb + gsw_ref[...].astype(F32) * y_sw
    x1 = x_ref[...] + jnp.dot(mixed.astype(BF16), wout_ref[...], preferred_element_type=F32)
    x1_ref[...] = x1
    ms = jnp.mean(x1 * x1, axis=-1, keepdims=True)
    h2 = x1 * lax.rsqrt(ms + EPS) * nrm_ref[...]
    half = h2.shape[1] // 2
    h2_ref[...] = _pack_bf16_pair(h2[:, :half], h2[:, half:])

    h_hi, h_lo = _split_bf16(h2)
    logits = (jnp.dot(h_hi, wr_hi_ref[...], preferred_element_type=F32)
              + jnp.dot(h_hi, wr_lo_ref[...], preferred_element_type=F32)
              + jnp.dot(h_lo, wr_hi_ref[...], preferred_element_type=F32)) + br_ref[...]
    lane = lax.broadcasted_iota(I32, logits.shape, 1)
    is_group = lane < N_GROUPS
    gmax, gidx = _first_index_of_max(logits, lane, is_group)
    gsum = jnp.sum(jnp.where(is_group, jnp.exp(logits - gmax), 0.0), axis=-1, keepdims=True)
    p_grp = 1.0 / gsum
    e_lane = lane - N_GROUPS
    in_grp = (e_lane >= gidx * EXPERTS_PER_GROUP) & (e_lane < (gidx + 1) * EXPERTS_PER_GROUP)
    m1, i1 = _first_index_of_max(logits, lane, in_grp)
    m2, i2 = _first_index_of_max(logits, lane, in_grp & (lane != i1))
    r = jnp.exp(m2 - m1)
    w1 = p_grp / (1.0 + r)
    w2 = p_grp * r / (1.0 + r)
    ri_ref[...] = jnp.where(lane == 0, i1 - N_GROUPS, jnp.where(lane == 1, i2 - N_GROUPS, 0))
    rw_ref[...] = jnp.where(lane == 0, w1, jnp.where(lane == 1, w2, 0.0))


def _mix_route(x2d, proj, o_sb, o_sw, w_sb, w_sw, w_out, nrm, wr_hi, wr_lo, br, *, tm):
    t, d = x2d.shape
    row = lambda i: (i, 0)
    const = lambda i: (0, 0)
    return pl.pallas_call(
        _mix_body,
        out_shape=(
            jax.ShapeDtypeStruct((t, d), F32),
            jax.ShapeDtypeStruct((t, d // 2), U32),
            jax.ShapeDtypeStruct((t, LANES), I32),
            jax.ShapeDtypeStruct((t, LANES), F32),
        ),
        grid=(t // tm,),
        in_specs=[
            pl.BlockSpec((tm, d), row),
            pl.BlockSpec((tm, SB_WIDTH), row),
            pl.BlockSpec((tm, SW_WIDTH), row),
            pl.BlockSpec((tm, d), lambda i: (i, COL_G_SB // d)),
            pl.BlockSpec((tm, d), lambda i: (i, COL_G_SW // d)),
            pl.BlockSpec(w_sb.shape, const),
            pl.BlockSpec(w_sw.shape, const),
            pl.BlockSpec(w_out.shape, const),
            pl.BlockSpec((1, d), const),
            pl.BlockSpec(wr_hi.shape, const),
            pl.BlockSpec(wr_lo.shape, const),
            pl.BlockSpec((1, LANES), const),
        ],
        out_specs=(
            pl.BlockSpec((tm, d), row),
            pl.BlockSpec((tm, d // 2), row),
            pl.BlockSpec((tm, LANES), row),
            pl.BlockSpec((tm, LANES), row),
        ),
        compiler_params=pltpu.CompilerParams(
            dimension_semantics=("parallel",), vmem_limit_bytes=VMEM_LIMIT),
        name="mix_route",
    )(x2d, o_sb, o_sw, proj, proj, w_sb, w_sw, w_out, nrm, wr_hi, wr_lo, br)


def _rank_body(ri_ref, ltri_ref, rank_ref, cnt_ref, carry):
    c = pl.program_id(0)

    @pl.when(c == 0)
    def _():
        carry[...] = jnp.zeros_like(carry)

    ri = ri_ref[...]
    lane = lax.broadcasted_iota(I32, ri.shape, 1)
    oh1 = lane == ri[:, 0:1]
    oh2 = lane == ri[:, 1:2]
    both = jnp.where(oh1 | oh2, 1.0, 0.0)
    before = jnp.dot(ltri_ref[...], both.astype(BF16), preferred_element_type=F32) + carry[...]
    r1 = jnp.sum(jnp.where(oh1, before, 0.0), axis=-1, keepdims=True)
    r2 = jnp.sum(jnp.where(oh2, before, 0.0), axis=-1, keepdims=True)
    rank_ref[...] = jnp.where(lane == 0, r1, jnp.where(lane == 1, r2, 0.0)).astype(I32)
    total = carry[...] + jnp.sum(both, axis=0, keepdims=True)
    carry[...] = total
    cnt_ref[...] = total.astype(I32)


def _ranks(route_i, ltri, *, chunk):
    t = route_i.shape[0]
    return pl.pallas_call(
        _rank_body,
        out_shape=(jax.ShapeDtypeStruct((t, LANES), I32), jax.ShapeDtypeStruct((1, LANES), I32)),
        grid=(t // chunk,),
        in_specs=[pl.BlockSpec((chunk, LANES), lambda c: (c, 0)),
                  pl.BlockSpec((chunk, chunk), lambda c: (0, 0))],
        out_specs=(pl.BlockSpec((chunk, LANES), lambda c: (c, 0)),
                   pl.BlockSpec((1, LANES), lambda c: (0, 0))),
        scratch_shapes=[pltpu.VMEM((1, LANES), F32)],
        compiler_params=pltpu.CompilerParams(dimension_semantics=("arbitrary",)),
        name="assignment_ranks",
    )(route_i, ltri)


def _dispatch_body(dest_ref, h_ref, rows_in, rows_out, sem, *, tm):
    del rows_in

    def row_copy(r, k):
        return pltpu.make_async_copy(
            h_ref.at[pl.ds(r, 1), :], rows_out.at[pl.ds(dest_ref[2 * r + k], 1), :], sem.at[0])

    def issue(r, carry):
        row_copy(r, 0).start()
        row_copy(r, 1).start()
        return carry

    lax.fori_loop(0, tm, issue, 0)

    def drain(r, carry):
        row_copy(r, 0).wait()
        row_copy(r, 1).wait()
        return carry

    lax.fori_loop(0, tm, drain, 0)


def _dispatch(dest_flat, h2p, rows_init, *, tm):
    t, dp = h2p.shape
    return pl.pallas_call(
        functools.partial(_dispatch_body, tm=tm),
        out_shape=jax.ShapeDtypeStruct(rows_init.shape, rows_init.dtype),
        grid=(t // tm,),
        in_specs=[
            pl.BlockSpec((TOP_K * tm,), lambda i: (i,), memory_space=pltpu.SMEM),
            pl.BlockSpec((tm, dp), lambda i: (i, 0)),
            pl.BlockSpec(memory_space=pl.ANY),
        ],
        out_specs=pl.BlockSpec(memory_space=pl.ANY),
        scratch_shapes=[pltpu.SemaphoreType.DMA((1,))],
        input_output_aliases={2: 0},
        compiler_params=pltpu.CompilerParams(
            dimension_semantics=("arbitrary",), has_side_effects=True),
        name="dispatch_rows",
    )(dest_flat, h2p, rows_init)


def _ffn_body(be_ref, nu_ref, x_ref, wg_ref, wu_ref, wd_ref, y_ref):
    j = pl.program_id(0)

    @pl.when(j < nu_ref[0])
    def _():
        x_lo, x_hi = _unpack_bf16_pair(x_ref[...])
        x_lo = x_lo.astype(BF16)
        x_hi = x_hi.astype(BF16)
        half = x_lo.shape[1]
        wg = wg_ref[0].astype(BF16)
        wu = wu_ref[0].astype(BF16)
        a = (jnp.dot(x_lo, wg[:half], preferred_element_type=F32)
             + jnp.dot(x_hi, wg[half:], preferred_element_type=F32))
        u = (jnp.dot(x_lo, wu[:half], preferred_element_type=F32)
             + jnp.dot(x_hi, wu[half:], preferred_element_type=F32))
        hmid = (a * _sigmoid(a) * u).astype(BF16)
        y = jnp.dot(hmid, wd_ref[0].astype(BF16), preferred_element_type=F32)
        y_ref[...] = _pack_bf16_pair(y[:, :half], y[:, half:])

    @pl.when(j >= nu_ref[0])
    def _():
        y_ref[...] = jnp.zeros_like(y_ref)


def _expert_ffn(block_e, n_used, x_rows, w_gate, w_up, w_down):
    n_rows, dp = x_rows.shape
    n_blocks = n_rows // ROWS_PER_BLOCK
    _, d, de = w_gate.shape

    def xmap(j, be, nu):
        return (jnp.minimum(j, nu[0] - 1), 0)

    def wmap(j, be, nu):
        return (be[jnp.minimum(j, nu[0] - 1)], 0, 0)

    grid_spec = pltpu.PrefetchScalarGridSpec(
        num_scalar_prefetch=2,
        grid=(n_blocks,),
        in_specs=[
            pl.BlockSpec((ROWS_PER_BLOCK, dp), xmap),
            pl.BlockSpec((1, d, de), wmap),
            pl.BlockSpec((1, d, de), wmap),
            pl.BlockSpec((1, de, d), wmap),
        ],
        out_specs=pl.BlockSpec((ROWS_PER_BLOCK, dp), lambda j, be, nu: (j, 0)),
    )
    return pl.pallas_call(
        _ffn_body,
        out_shape=jax.ShapeDtypeStruct((n_rows, dp), U32),
        grid_spec=grid_spec,
        compiler_params=pltpu.CompilerParams(
            dimension_semantics=("arbitrary",), vmem_limit_bytes=VMEM_LIMIT),
        name="expert_ffn",
    )(block_e, n_used, x_rows, w_gate, w_up, w_down)


def _combine_body(dest_ref, x1_ref, rw_ref, nrm_ref, y_hbm, o_ref, ybuf, sem, *, tm):
    def row_copy(r, k):
        return pltpu.make_async_copy(
            y_hbm.at[pl.ds(dest_ref[2 * r + k], 1), :], ybuf.at[k, pl.ds(r, 1), :], sem.at[0])

    def issue(r, carry):
        row_copy(r, 0).start()
        row_copy(r, 1).start()
        return carry

    lax.fori_loop(0, tm, issue, 0)

    def drain(r, carry):
        row_copy(r, 0).wait()
        row_copy(r, 1).wait()
        return carry

    lax.fori_loop(0, tm, drain, 0)

    x1 = x1_ref[...]
    half = x1.shape[1] // 2
    rw = rw_ref[...]
    w1 = rw[:, 0:1]
    w2 = rw[:, 1:2]
    a_lo, a_hi = _unpack_bf16_pair(ybuf[0])
    b_lo, b_hi = _unpack_bf16_pair(ybuf[1])
    out_lo = x1[:, :half] + (a_lo * w1 + b_lo * w2)
    out_hi = x1[:, half:] + (a_hi * w1 + b_hi * w2)
    ms = (jnp.sum(out_lo * out_lo, axis=-1, keepdims=True)
          + jnp.sum(out_hi * out_hi, axis=-1, keepdims=True)) / x1.shape[1]
    inv = lax.rsqrt(ms + EPS)
    g = nrm_ref[...]
    o_ref[:, :half] = out_lo * inv * g[:, :half]
    o_ref[:, half:] = out_hi * inv * g[:, half:]


def _combine(dest_flat, x1, route_w, nrm, y_rows, *, tm):
    t, d = x1.shape
    return pl.pallas_call(
        functools.partial(_combine_body, tm=tm),
        out_shape=jax.ShapeDtypeStruct((t, d), F32),
        grid=(t // tm,),
        in_specs=[
            pl.BlockSpec((TOP_K * tm,), lambda i: (i,), memory_space=pltpu.SMEM),
            pl.BlockSpec((tm, d), lambda i: (i, 0)),
            pl.BlockSpec((tm, LANES), lambda i: (i, 0)),
            pl.BlockSpec((1, d), lambda i: (0, 0)),
            pl.BlockSpec(memory_space=pl.ANY),
        ],
        out_specs=pl.BlockSpec((tm, d), lambda i: (i, 0)),
        scratch_shapes=[pltpu.VMEM((TOP_K, tm, d // 2), U32), pltpu.SemaphoreType.DMA((1,))],
        compiler_params=pltpu.CompilerParams(
            dimension_semantics=("arbitrary",), vmem_limit_bytes=VMEM_LIMIT),
        name="combine_norm",
    )(dest_flat, x1, route_w, nrm, y_rows)


def _strict_lower_tri(n, dtype):
    r = lax.broadcasted_iota(I32, (n, n), 0)
    c = lax.broadcasted_iota(I32, (n, n), 1)
    return (r > c).astype(dtype)


def _pick_tile(n, pref):
    tile = pref
    while n % tile:
        tile //= 2
    return tile


def kernel(x, norm_mix, w_in, sinks, w_up_sb, w_up_sw, w_out, norm_ffn, w_router_group,
           b_router_group, w_router_expert, b_router_expert, w_gate, w_up, w_down, norm_final):
    batch, seq_len, d = x.shape
    t = batch * seq_len
    depth = norm_mix.shape[0]
    assert depth == 1, "the final RMSNorm is fused into the only layer's combine stage"
    assert d == 2048 and seq_len % BLOCK == 0 and t % 256 == 0

    tri = _strict_lower_tri(SB_WINDOW_BLOCKS * BLOCK, BF16)
    tri2 = jnp.concatenate([tri, tri], axis=0)
    rank_chunk = _pick_tile(t, 1024)
    ltri = _strict_lower_tri(rank_chunk, BF16)

    n_assign = t * TOP_K
    n_blocks = -(-n_assign // ROWS_PER_BLOCK) + N_EXPERTS
    n_rows = n_blocks * ROWS_PER_BLOCK

    x2d = x.reshape(t, d)
    for layer in range(depth):
        wl = w_in[layer]
        o = 0
        segs = {}
        for name, width in (("q_sb", SB_WIDTH), ("k_sb", SB_WIDTH), ("v_sb", SB_WIDTH),
                            ("q_sw", SW_WIDTH), ("k_sw", SW_KV_WIDTH), ("v_sw", SW_KV_WIDTH),
                            ("g_sb", d), ("g_sw", d)):
            segs[name] = wl[:, o:o + width]
            o += width
        w_perm = jnp.concatenate(
            [segs[n] for n in ("g_sb", "g_sw", "q_sw", "q_sb", "k_sb", "v_sb", "k_sw", "v_sw")],
            axis=1).astype(BF16)

        proj = _inproj(x2d, norm_mix[layer][None, :], w_perm,
                       tm=_pick_tile(t, 1024), tn=768, n_gate_cols=2 * d)
        o_sb = _sb_attention(proj, tri2, batch=batch, seq_len=seq_len)
        o_sw = _sw_attention(proj, sinks[layer], batch=batch, seq_len=seq_len)

        w_r = jnp.concatenate([w_router_group[layer], w_router_expert[layer]], axis=1)
        w_r = jnp.pad(w_r, ((0, 0), (0, LANES - w_r.shape[1])))
        wr_hi = w_r.astype(BF16)
        wr_lo = (w_r - wr_hi.astype(F32)).astype(BF16)
        b_r = jnp.concatenate([b_router_group[layer], b_router_expert[layer]])
        b_r = jnp.pad(b_r, (0, LANES - b_r.shape[0]))[None, :]

        x1, h2p, route_i, route_w = _mix_route(
            x2d, proj, o_sb, o_sw, w_up_sb[layer].astype(BF16), w_up_sw[layer].astype(BF16),
            w_out[layer].astype(BF16), norm_ffn[layer][None, :], wr_hi, wr_lo, b_r, tm=256)

        rank, counts = _ranks(route_i, ltri, chunk=rank_chunk)

        counts = counts[0, :N_EXPERTS]
        padded = (counts + ROWS_PER_BLOCK - 1) // ROWS_PER_BLOCK * ROWS_PER_BLOCK
        pend = jnp.cumsum(padded)
        pstart = pend - padded
        dest = (pstart[route_i[:, :TOP_K]] + rank[:, :TOP_K]).reshape(-1).astype(I32)
        block_e = jnp.clip(
            jnp.searchsorted(pend, jnp.arange(n_blocks, dtype=I32) * ROWS_PER_BLOCK, side="right"),
            0, N_EXPERTS - 1).astype(I32)
        n_used = (pend[-1] // ROWS_PER_BLOCK).astype(I32)[None]

        x_rows = _dispatch(dest, h2p, jnp.zeros((n_rows, d // 2), U32), tm=256)
        y_rows = _expert_ffn(block_e, n_used, x_rows, w_gate[layer], w_up[layer], w_down[layer])
        x2d = _combine(dest, x1, route_w, norm_final[None, :], y_rows, tm=256)
    return x2d.reshape(batch, seq_len, d)
```

```python
import functools
import math

import jax
import jax.numpy as jnp
from jax import lax
from jax.experimental import pallas as pl
from jax.experimental.pallas import tpu as pltpu

F32 = jnp.float32
BF16 = jnp.bfloat16
U32 = jnp.uint32
I32 = jnp.int32

EPS = 1e-6
BLOCK = 128
SB_HEADS = 4
SB_HEAD_DIM = 128
SB_WIDTH = SB_HEADS * SB_HEAD_DIM
SW_HEADS = 16
SW_KV_HEADS = 2
SW_HEAD_DIM = 64
SW_WIDTH = SW_HEADS * SW_HEAD_DIM
SW_KV_WIDTH = SW_KV_HEADS * SW_HEAD_DIM
WINDOW = 128
N_GROUPS = 4
EXPERTS_PER_GROUP = 8
N_EXPERTS = N_GROUPS * EXPERTS_PER_GROUP
TOP_K = 2
ROWS_PER_BLOCK = 512
NEG_INF = -1e30
LANES = 128

COL_G_SB = 0
COL_G_SW = 2048
COL_Q_SW = 4096
COL_Q_SB = 5120
COL_K_SB = 5632
COL_V_SB = 6144
COL_K_SW = 6656
COL_V_SW = 6784

SB_EXIT_LOG = -110.0
SB_WINDOW_BLOCKS = 3

VMEM_LIMIT = 48 * 1024 * 1024


def _sigmoid(v):
    return 1.0 / (1.0 + jnp.exp(-v))


def _pack_bf16_pair(lo, hi):
    lo_bits = lax.bitcast_convert_type(lo.astype(BF16).astype(F32), U32)
    hi_bits = lax.bitcast_convert_type(hi.astype(BF16).astype(F32), U32)
    return (lo_bits >> 16) | hi_bits


def _unpack_bf16_pair(w):
    lo = lax.bitcast_convert_type(w << 16, F32)
    hi = lax.bitcast_convert_type(w & jnp.uint32(0xFFFF0000), F32)
    return lo, hi


SUBLANES = 8


def _rows_to_tiles(rows):
    return pltpu.einshape("m(cl)->(mc)l", rows, c=SUBLANES)


def _tiles_to_rows(tiles):
    return pltpu.einshape("(mc)l->m(cl)", tiles, c=SUBLANES)


def _inproj_body(x_ref, g_ref, w_ref, o_ref, h_scr, *, tn, n_gate_cols):
    j = pl.program_id(1)

    @pl.when(j == 0)
    def _():
        xf = x_ref[...]
        ms = jnp.mean(xf * xf, axis=-1, keepdims=True)
        h_scr[...] = (xf * lax.rsqrt(ms + EPS) * g_ref[...]).astype(BF16)

    acc = jnp.dot(h_scr[...], w_ref[...], preferred_element_type=F32)
    n_full, rem = divmod(n_gate_cols, tn)

    @pl.when(j < n_full)
    def _():
        o_ref[...] = _sigmoid(acc).astype(o_ref.dtype)

    if rem:
        @pl.when(j == n_full)
        def _():
            o_ref[:, :rem] = _sigmoid(acc[:, :rem]).astype(o_ref.dtype)
            o_ref[:, rem:] = acc[:, rem:].astype(o_ref.dtype)

    @pl.when(j >= n_full + (1 if rem else 0))
    def _():
        o_ref[...] = acc.astype(o_ref.dtype)


def _inproj(x2d, gain, w_perm, *, tm, tn, n_gate_cols):
    t, d = x2d.shape
    n = w_perm.shape[1]
    return pl.pallas_call(
        functools.partial(_inproj_body, tn=tn, n_gate_cols=n_gate_cols),
        out_shape=jax.ShapeDtypeStruct((t, n), BF16),
        grid=(t // tm, n // tn),
        in_specs=[
            pl.BlockSpec((tm, d), lambda i, j: (i, 0)),
            pl.BlockSpec((1, d), lambda i, j: (0, 0)),
            pl.BlockSpec((d, tn), lambda i, j: (0, j)),
        ],
        out_specs=pl.BlockSpec((tm, tn), lambda i, j: (i, j)),
        scratch_shapes=[pltpu.VMEM((tm, d), BF16)],
        compiler_params=pltpu.CompilerParams(
            dimension_semantics=("parallel", "arbitrary"), vmem_limit_bytes=VMEM_LIMIT),
        name="inproj",
    )(x2d, gain, w_perm)


def _sb_terms(z):
    e = jnp.log(1.0 + jnp.exp(-jnp.abs(z)))
    return jnp.minimum(z, 0.0) - e, -jnp.maximum(z, 0.0) - e


def _split_bf16(v):
    hi = v.astype(BF16)
    lo = (v - hi.astype(F32)).astype(BF16)
    return hi, lo


def _sb_body(q_ref, k0_ref, k1_ref, k2_ref, v0_ref, v1_ref, v2_ref, tri_ref, proj_hbm,
             o_ref, acc_scr, run_scr, kbuf, vbuf, sem, *, seq_len):
    b = pl.program_id(0)
    i = pl.program_id(1)
    scale = 1.0 / math.sqrt(SB_HEAD_DIM)
    wkeys = SB_WINDOW_BLOCKS * BLOCK
    ws = jnp.maximum(i - (SB_WINDOW_BLOCKS - 1), 0)
    qpos = i * BLOCK + lax.broadcasted_iota(I32, (BLOCK, wkeys), 0)
    kpos = ws * BLOCK + lax.broadcasted_iota(I32, (BLOCK, wkeys), 1)
    mask = kpos < qpos
    tri_hi = tri_ref[...]

    run_max = None
    for h in range(SB_HEADS):
        hs = slice(h * SB_HEAD_DIM, (h + 1) * SB_HEAD_DIM)
        q = q_ref[:, hs]
        k = jnp.concatenate([k0_ref[:, hs], k1_ref[:, hs], k2_ref[:, hs]], axis=0)
        v = jnp.concatenate([v0_ref[:, hs], v1_ref[:, hs], v2_ref[:, hs]], axis=0)
        z = lax.dot_general(q, k, (((1,), (1,)), ((), ())), preferred_element_type=F32) * scale
        log_beta, log_fail = _sb_terms(z)
        lf = jnp.where(mask, log_fail, 0.0)
        hi, lo = _split_bf16(lf)
        later = jnp.dot(jnp.concatenate([hi, lo], axis=1), tri_hi, preferred_element_type=F32)
        a = jnp.where(mask, jnp.exp(log_beta + later), 0.0)
        acc_scr[:, hs] = jnp.dot(a.astype(BF16), v, preferred_element_type=F32)
        run = jnp.sum(lf, axis=1, keepdims=True)
        run_scr[:, h:h + 1] = run
        m = jnp.max(run)
        run_max = m if run_max is None else jnp.maximum(run_max, m)

    tri128 = tri_ref[0:BLOCK, 0:BLOCK]

    def tail_cond(carry):
        kb, rmax = carry
        return jnp.logical_and(kb >= 0, rmax > SB_EXIT_LOG)

    def tail_body(carry):
        kb, _ = carry
        row0 = pl.multiple_of(b * seq_len + kb * BLOCK, BLOCK)
        ck = pltpu.make_async_copy(
            proj_hbm.at[pl.ds(row0, BLOCK), pl.ds(COL_K_SB, SB_WIDTH)], kbuf, sem.at[0])
        cv = pltpu.make_async_copy(
            proj_hbm.at[pl.ds(row0, BLOCK), pl.ds(COL_V_SB, SB_WIDTH)], vbuf, sem.at[1])
        ck.start()
        cv.start()
        ck.wait()
        cv.wait()
        new_max = None
        for h in range(SB_HEADS):
            hs = slice(h * SB_HEAD_DIM, (h + 1) * SB_HEAD_DIM)
            z = lax.dot_general(q_ref[:, hs], kbuf[:, hs], (((1,), (1,)), ((), ())),
                                preferred_element_type=F32) * scale
            log_beta, lf = _sb_terms(z)
            hi, lo = _split_bf16(lf)
            run = run_scr[:, h:h + 1]
            later = (jnp.dot(hi, tri128, preferred_element_type=F32)
                     + jnp.dot(lo, tri128, preferred_element_type=F32)) + run
            a = jnp.exp(log_beta + later)
            acc_scr[:, hs] += jnp.dot(a.astype(BF16), vbuf[:, hs], preferred_element_type=F32)
            run = run + jnp.sum(lf, axis=1, keepdims=True)
            run_scr[:, h:h + 1] = run
            m = jnp.max(run)
            new_max = m if new_max is None else jnp.maximum(new_max, m)
        return kb - 1, new_max

    lax.while_loop(tail_cond, tail_body, (ws - 1, run_max))
    o_ref[...] = acc_scr[...].astype(o_ref.dtype)


def _sb_attention(proj, tri, *, batch, seq_len):
    nb = seq_len // BLOCK
    assert nb >= SB_WINDOW_BLOCKS
    t = proj.shape[0]
    wcol = SB_WIDTH

    def qmap(b, i):
        return (b * nb + i, COL_Q_SB // wcol)

    def kvmap(col, j):
        def f(b, i):
            return (b * nb + jnp.maximum(i - (SB_WINDOW_BLOCKS - 1), 0) + j, col // wcol)
        return f

    blk = (BLOCK, wcol)
    in_specs = [pl.BlockSpec(blk, qmap)]
    in_specs += [pl.BlockSpec(blk, kvmap(COL_K_SB, j)) for j in range(SB_WINDOW_BLOCKS)]
    in_specs += [pl.BlockSpec(blk, kvmap(COL_V_SB, j)) for j in range(SB_WINDOW_BLOCKS)]
    in_specs += [pl.BlockSpec(tri.shape, lambda b, i: (0, 0)),
                 pl.BlockSpec(memory_space=pl.ANY)]
    return pl.pallas_call(
        functools.partial(_sb_body, seq_len=seq_len),
        out_shape=jax.ShapeDtypeStruct((t, SB_WIDTH), BF16),
        grid=(batch, nb),
        in_specs=in_specs,
        out_specs=pl.BlockSpec((BLOCK, SB_WIDTH), lambda b, i: (b * nb + i, 0)),
        scratch_shapes=[
            pltpu.VMEM((BLOCK, SB_WIDTH), F32),
            pltpu.VMEM((BLOCK, LANES), F32),
            pltpu.VMEM((BLOCK, SB_WIDTH), BF16),
            pltpu.VMEM((BLOCK, SB_WIDTH), BF16),
            pltpu.SemaphoreType.DMA((2,)),
        ],
        compiler_params=pltpu.CompilerParams(
            dimension_semantics=("parallel", "arbitrary"), vmem_limit_bytes=VMEM_LIMIT),
        name="sb_attention",
    )(proj, proj, proj, proj, proj, proj, proj, tri, proj)


def _sw_body(sink_ref, q_ref, kp_ref, kc_ref, vp_ref, vc_ref, o_ref):
    i = pl.program_id(1)
    scale = 1.0 / math.sqrt(SW_HEAD_DIM)
    group = SW_HEADS // SW_KV_HEADS
    qi = lax.broadcasted_iota(I32, (BLOCK, 2 * BLOCK), 0)
    kj = lax.broadcasted_iota(I32, (BLOCK, 2 * BLOCK), 1)
    dist = qi + BLOCK - kj
    key_abs = (i - 1) * BLOCK + kj
    mask = (dist >= 0) & (dist < WINDOW) & (key_abs >= 0)
    distf = dist.astype(F32)
    for c in range(SW_KV_HEADS):
        cs = slice(c * SW_HEAD_DIM, (c + 1) * SW_HEAD_DIM)
        k = jnp.concatenate([kp_ref[:, cs], kc_ref[:, cs]], axis=0)
        v = jnp.concatenate([vp_ref[:, cs], vc_ref[:, cs]], axis=0)
        for g in range(group):
            h = c * group + g
            slope = 2.0 ** (-8.0 * (h + 1) / SW_HEADS)
            hs = slice(h * SW_HEAD_DIM, (h + 1) * SW_HEAD_DIM)
            sc = lax.dot_general(q_ref[:, hs], k, (((1,), (1,)), ((), ())),
                                 preferred_element_type=F32) * scale
            sc = jnp.where(mask, sc - slope * distf, NEG_INF)
            sink = sink_ref[h]
            m = jnp.maximum(jnp.max(sc, axis=-1, keepdims=True), sink)
            p = jnp.exp(sc - m)
            denom = jnp.sum(p, axis=-1, keepdims=True) + jnp.exp(sink - m)
            o = jnp.dot(p.astype(BF16), v, preferred_element_type=F32) / denom
            o_ref[:, hs] = o.astype(o_ref.dtype)


def _sw_attention(proj, sinks, *, batch, seq_len):
    nb = seq_len // BLOCK
    t = proj.shape[0]

    def prev(col):
        return lambda b, i, s: (b * nb + jnp.maximum(i - 1, 0), col // SW_KV_WIDTH)

    def cur(col):
        return lambda b, i, s: (b * nb + i, col // SW_KV_WIDTH)

    kvblk = (BLOCK, SW_KV_WIDTH)
    grid_spec = pltpu.PrefetchScalarGridSpec(
        num_scalar_prefetch=1,
        grid=(batch, nb),
        in_specs=[
            pl.BlockSpec((BLOCK, SW_WIDTH), lambda b, i, s: (b * nb + i, COL_Q_SW // SW_WIDTH)),
            pl.BlockSpec(kvblk, prev(COL_K_SW)),
            pl.BlockSpec(kvblk, cur(COL_K_SW)),
            pl.BlockSpec(kvblk, prev(COL_V_SW)),
            pl.BlockSpec(kvblk, cur(COL_V_SW)),
        ],
        out_specs=pl.BlockSpec((BLOCK, SW_WIDTH), lambda b, i, s: (b * nb + i, 0)),
    )
    return pl.pallas_call(
        _sw_body,
        out_shape=jax.ShapeDtypeStruct((t, SW_WIDTH), BF16),
        grid_spec=grid_spec,
        compiler_params=pltpu.CompilerParams(
            dimension_semantics=("parallel", "parallel"), vmem_limit_bytes=VMEM_LIMIT),
        name="sw_attention",
    )(sinks, proj, proj, proj, proj, proj)


def _first_index_of_max(vals, lane, valid):
    masked = jnp.where(valid, vals, jnp.finfo(F32).min)
    m = jnp.max(masked, axis=-1, keepdims=True)
    lane_f = lane.astype(F32)
    idx = jnp.min(jnp.where(valid & (masked == m), lane_f, float(LANES)), axis=-1, keepdims=True)
    return m, idx.astype(I32)


def _mix_body(x_ref, osb_ref, osw_ref, gsb_ref, gsw_ref, wsb_ref, wsw_ref, wout_ref,
              nrm_ref, wr_hi_ref, wr_lo_ref, br_ref,
              x1_ref, h2_ref, ri_ref, rw_ref):
    y_sb = jnp.dot(osb_ref[...], wsb_ref[...], preferred_element_type=F32)
    y_sw = jnp.dot(osw_ref[...], wsw_ref[...], preferred_element_type=F32)
    mixed = gsb_ref[...].astype(F32) * y_sb + gsw_ref[...].astype(F32) * y_sw
    x1 = x_ref[...] + jnp.dot(mixed.astype(BF16), wout_ref[...], preferred_element_type=F32)
    x1_ref[...] = x1
    ms = jnp.mean(x1 * x1, axis=-1, keepdims=True)
    h2 = x1 * lax.rsqrt(ms + EPS) * nrm_ref[...]
    half = h2.shape[1] // 2
    h2_ref[...] = _rows_to_tiles(_pack_bf16_pair(h2[:, :half], h2[:, half:]))

    h_hi, h_lo = _split_bf16(h2)
    logits = (jnp.dot(h_hi, wr_hi_ref[...], preferred_element_type=F32)
              + jnp.dot(h_hi, wr_lo_ref[...], preferred_element_type=F32)
              + jnp.dot(h_lo, wr_hi_ref[...], preferred_element_type=F32)) + br_ref[...]
    lane = lax.broadcasted_iota(I32, logits.shape, 1)
    is_group = lane < N_GROUPS
    gmax, gidx = _first_index_of_max(logits, lane, is_group)
    gsum = jnp.sum(jnp.where(is_group, jnp.exp(logits - gmax), 0.0), axis=-1, keepdims=True)
    p_grp = 1.0 / gsum
    e_lane = lane - N_GROUPS
    in_grp = (e_lane >= gidx * EXPERTS_PER_GROUP) & (e_lane < (gidx + 1) * EXPERTS_PER_GROUP)
    m1, i1 = _first_index_of_max(logits, lane, in_grp)
    m2, i2 = _first_index_of_max(logits, lane, in_grp & (lane != i1))
    r = jnp.exp(m2 - m1)
    w1 = p_grp / (1.0 + r)
    w2 = p_grp * r / (1.0 + r)
    ri_ref[...] = jnp.where(lane == 0, i1 - N_GROUPS, jnp.where(lane == 1, i2 - N_GROUPS, 0))
    rw_ref[...] = jnp.where(lane == 0, w1, jnp.where(lane == 1, w2, 0.0))


def _mix_route(x2d, proj, o_sb, o_sw, w_sb, w_sw, w_out, nrm, wr_hi, wr_lo, br, *, tm):
    t, d = x2d.shape
    row = lambda i: (i, 0)
    const = lambda i: (0, 0)
    return pl.pallas_call(
        _mix_body,
        out_shape=(
            jax.ShapeDtypeStruct((t, d), F32),
            jax.ShapeDtypeStruct((t * SUBLANES, LANES), U32),
            jax.ShapeDtypeStruct((t, LANES), I32),
            jax.ShapeDtypeStruct((t, LANES), F32),
        ),
        grid=(t // tm,),
        in_specs=[
            pl.BlockSpec((tm, d), row),
            pl.BlockSpec((tm, SB_WIDTH), row),
            pl.BlockSpec((tm, SW_WIDTH), row),
            pl.BlockSpec((tm, d), lambda i: (i, COL_G_SB // d)),
            pl.BlockSpec((tm, d), lambda i: (i, COL_G_SW // d)),
            pl.BlockSpec(w_sb.shape, const),
            pl.BlockSpec(w_sw.shape, const),
            pl.BlockSpec(w_out.shape, const),
            pl.BlockSpec((1, d), const),
            pl.BlockSpec(wr_hi.shape, const),
            pl.BlockSpec(wr_lo.shape, const),
            pl.BlockSpec((1, LANES), const),
        ],
        out_specs=(
            pl.BlockSpec((tm, d), row),
            pl.BlockSpec((tm * SUBLANES, LANES), row),
            pl.BlockSpec((tm, LANES), row),
            pl.BlockSpec((tm, LANES), row),
        ),
        compiler_params=pltpu.CompilerParams(
            dimension_semantics=("parallel",), vmem_limit_bytes=VMEM_LIMIT),
        name="mix_route",
    )(x2d, o_sb, o_sw, proj, proj, w_sb, w_sw, w_out, nrm, wr_hi, wr_lo, br)


def _rank_body(ri_ref, ltri_ref, rank_ref, cnt_ref, carry):
    c = pl.program_id(0)

    @pl.when(c == 0)
    def _():
        carry[...] = jnp.zeros_like(carry)

    ri = ri_ref[...]
    lane = lax.broadcasted_iota(I32, ri.shape, 1)
    oh1 = lane == ri[:, 0:1]
    oh2 = lane == ri[:, 1:2]
    both = jnp.where(oh1 | oh2, 1.0, 0.0)
    before = jnp.dot(ltri_ref[...], both.astype(BF16), preferred_element_type=F32) + carry[...]
    r1 = jnp.sum(jnp.where(oh1, before, 0.0), axis=-1, keepdims=True)
    r2 = jnp.sum(jnp.where(oh2, before, 0.0), axis=-1, keepdims=True)
    rank_ref[...] = jnp.where(lane == 0, r1, jnp.where(lane == 1, r2, 0.0)).astype(I32)
    total = carry[...] + jnp.sum(both, axis=0, keepdims=True)
    carry[...] = total
    cnt_ref[...] = total.astype(I32)


def _ranks(route_i, ltri, *, chunk):
    t = route_i.shape[0]
    return pl.pallas_call(
        _rank_body,
        out_shape=(jax.ShapeDtypeStruct((t, LANES), I32), jax.ShapeDtypeStruct((1, LANES), I32)),
        grid=(t // chunk,),
        in_specs=[pl.BlockSpec((chunk, LANES), lambda c: (c, 0)),
                  pl.BlockSpec((chunk, chunk), lambda c: (0, 0))],
        out_specs=(pl.BlockSpec((chunk, LANES), lambda c: (c, 0)),
                   pl.BlockSpec((1, LANES), lambda c: (0, 0))),
        scratch_shapes=[pltpu.VMEM((1, LANES), F32)],
        compiler_params=pltpu.CompilerParams(dimension_semantics=("arbitrary",)),
        name="assignment_ranks",
    )(route_i, ltri)


DMA_UNROLL = 8


def _token_tile(ref, first_row):
    return ref.at[pl.ds(pl.multiple_of(first_row, SUBLANES), SUBLANES), :]


def _dispatch_body(pad_start_ref, pad_len_ref, nu_ref, dest_ref, h_hbm, rows_out, zblk, sem, *, tm):
    i = pl.program_id(0)
    n_steps = pl.num_programs(0)
    blk_rows = ROWS_PER_BLOCK * SUBLANES
    n_blocks = rows_out.shape[0] // blk_rows
    zrow = zblk.at[pl.ds(0, SUBLANES), :]

    def tail_copy(j):
        return pltpu.make_async_copy(
            zblk, rows_out.at[pl.ds(pl.multiple_of(j * blk_rows, blk_rows), blk_rows), :],
            sem.at[TOP_K + 1])

    def issue(g, carry):
        n = DMA_UNROLL * TOP_K
        dst = [dest_ref[g * n + j] for j in range(n)]
        for s in range(DMA_UNROLL):
            src = _token_tile(h_hbm, (i * tm + g * DMA_UNROLL + s) * SUBLANES)
            for k in range(TOP_K):
                pltpu.make_async_copy(
                    src, _token_tile(rows_out, dst[TOP_K * s + k]), sem.at[k]).start(priority=k)
        return carry

    lax.fori_loop(0, tm // DMA_UNROLL, issue, 0)

    def pad_copy(e, r):
        return pltpu.make_async_copy(
            zrow, _token_tile(rows_out, pad_start_ref[e] + r * SUBLANES), sem.at[TOP_K])

    @pl.when(i == 0)
    def _():
        zblk[...] = jnp.zeros_like(zblk)

        def per_expert(e, carry):
            lax.fori_loop(0, pad_len_ref[e], lambda r, c: (pad_copy(e, r).start(), c)[1], 0)
            return carry

        lax.fori_loop(0, N_EXPERTS, per_expert, 0)
        lax.fori_loop(nu_ref[0], n_blocks, lambda j, c: (tail_copy(j).start(), c)[1], 0)

    def wait_tile():
        span = pl.ds(0, tm * SUBLANES)
        for k in range(TOP_K):
            pltpu.make_async_copy(h_hbm.at[span, :], rows_out.at[span, :], sem.at[k]).wait()

    @pl.when(i > 0)
    def _():
        wait_tile()

    @pl.when(i == n_steps - 1)
    def _():
        wait_tile()

        def per_expert(e, carry):
            lax.fori_loop(0, pad_len_ref[e], lambda r, c: (pad_copy(e, r).wait(), c)[1], 0)
            return carry

        lax.fori_loop(0, N_EXPERTS, per_expert, 0)
        lax.fori_loop(nu_ref[0], n_blocks, lambda j, c: (tail_copy(j).wait(), c)[1], 0)


def _dispatch(pad_start, pad_len, n_used, dest_flat, h2p, n_rows, *, tm):
    t = h2p.shape[0] // SUBLANES
    grid_spec = pltpu.PrefetchScalarGridSpec(
        num_scalar_prefetch=3,
        grid=(t // tm,),
        in_specs=[
            pl.BlockSpec((TOP_K * tm,), lambda i, ps, pn, nu: (i,), memory_space=pltpu.SMEM),
            pl.BlockSpec(memory_space=pl.ANY),
        ],
        out_specs=pl.BlockSpec(memory_space=pl.ANY),
        scratch_shapes=[pltpu.VMEM((ROWS_PER_BLOCK * SUBLANES, LANES), U32),
                        pltpu.SemaphoreType.DMA((TOP_K + 2,))],
    )
    return pl.pallas_call(
        functools.partial(_dispatch_body, tm=tm),
        out_shape=jax.ShapeDtypeStruct((n_rows * SUBLANES, LANES), U32),
        grid_spec=grid_spec,
        compiler_params=pltpu.CompilerParams(
            dimension_semantics=("arbitrary",), has_side_effects=True),
        name="dispatch_rows",
    )(pad_start, pad_len, n_used, dest_flat, h2p)


def _ffn_body(be_ref, nu_ref, x_ref, wg_ref, wu_ref, wd_ref, y_ref):
    j = pl.program_id(0)

    @pl.when(j < nu_ref[0])
    def _():
        x_lo, x_hi = _unpack_bf16_pair(_tiles_to_rows(x_ref[...]))
        x_lo = x_lo.astype(BF16)
        x_hi = x_hi.astype(BF16)
        half = x_lo.shape[1]
        wg = wg_ref[0].astype(BF16)
        wu = wu_ref[0].astype(BF16)
        a = (jnp.dot(x_lo, wg[:half], preferred_element_type=F32)
             + jnp.dot(x_hi, wg[half:], preferred_element_type=F32))
        u = (jnp.dot(x_lo, wu[:half], preferred_element_type=F32)
             + jnp.dot(x_hi, wu[half:], preferred_element_type=F32))
        hmid = (a * _sigmoid(a) * u).astype(BF16)
        y = jnp.dot(hmid, wd_ref[0].astype(BF16), preferred_element_type=F32)
        y_ref[...] = _rows_to_tiles(_pack_bf16_pair(y[:, :half], y[:, half:]))

    @pl.when(j >= nu_ref[0])
    def _():
        y_ref[...] = jnp.zeros_like(y_ref)


def _expert_ffn(block_e, n_used, x_rows, w_gate, w_up, w_down):
    n_blocks = x_rows.shape[0] // (ROWS_PER_BLOCK * SUBLANES)
    _, d, de = w_gate.shape
    blk = (ROWS_PER_BLOCK * SUBLANES, LANES)

    def xmap(j, be, nu):
        return (jnp.minimum(j, nu[0] - 1), 0)

    def wmap(j, be, nu):
        return (be[jnp.minimum(j, nu[0] - 1)], 0, 0)

    grid_spec = pltpu.PrefetchScalarGridSpec(
        num_scalar_prefetch=2,
        grid=(n_blocks,),
        in_specs=[
            pl.BlockSpec(blk, xmap),
            pl.BlockSpec((1, d, de), wmap),
            pl.BlockSpec((1, d, de), wmap),
            pl.BlockSpec((1, de, d), wmap),
        ],
        out_specs=pl.BlockSpec(blk, lambda j, be, nu: (j, 0)),
    )
    return pl.pallas_call(
        _ffn_body,
        out_shape=jax.ShapeDtypeStruct(x_rows.shape, U32),
        grid_spec=grid_spec,
        compiler_params=pltpu.CompilerParams(
            dimension_semantics=("arbitrary",), vmem_limit_bytes=VMEM_LIMIT),
        name="expert_ffn",
    )(block_e, n_used, x_rows, w_gate, w_up, w_down)


def _combine_body(dest_ref, dest_next_ref, x1_ref, rw_ref, nrm_ref, y_hbm, o_ref, ybuf, sem, *, tm):
    i = pl.program_id(0)
    n_steps = pl.num_programs(0)
    slot = lax.rem(i, 2)

    def issue(dref, slot_):
        def group(g, carry):
            n = DMA_UNROLL * TOP_K
            src = [dref[g * n + j] for j in range(n)]
            for s in range(DMA_UNROLL):
                r = g * DMA_UNROLL + s
                for k in range(TOP_K):
                    pltpu.make_async_copy(
                        _token_tile(y_hbm, src[TOP_K * s + k]),
                        _token_tile(ybuf.at[slot_, k], r * SUBLANES),
                        sem.at[slot_]).start(priority=k)
            return carry

        lax.fori_loop(0, tm // DMA_UNROLL, group, 0)

    @pl.when(i == 0)
    def _():
        issue(dest_ref, slot)

    @pl.when(i + 1 < n_steps)
    def _():
        issue(dest_next_ref, 1 - slot)

    pltpu.make_async_copy(ybuf.at[slot], ybuf.at[slot], sem.at[slot]).wait()

    x1 = x1_ref[...]
    half = x1.shape[1] // 2
    rw = rw_ref[...]
    w1 = rw[:, 0:1]
    w2 = rw[:, 1:2]
    a_lo, a_hi = _unpack_bf16_pair(_tiles_to_rows(ybuf[slot, 0]))
    b_lo, b_hi = _unpack_bf16_pair(_tiles_to_rows(ybuf[slot, 1]))
    out_lo = x1[:, :half] + (a_lo * w1 + b_lo * w2)
    out_hi = x1[:, half:] + (a_hi * w1 + b_hi * w2)
    ms = (jnp.sum(out_lo * out_lo, axis=-1, keepdims=True)
          + jnp.sum(out_hi * out_hi, axis=-1, keepdims=True)) / x1.shape[1]
    inv = lax.rsqrt(ms + EPS)
    g = nrm_ref[...]
    o_ref[:, :half] = out_lo * inv * g[:, :half]
    o_ref[:, half:] = out_hi * inv * g[:, half:]


def _combine(dest_flat, x1, route_w, nrm, y_rows, *, tm):
    t, d = x1.shape
    n_steps = t // tm
    return pl.pallas_call(
        functools.partial(_combine_body, tm=tm),
        out_shape=jax.ShapeDtypeStruct((t, d), F32),
        grid=(n_steps,),
        in_specs=[
            pl.BlockSpec((TOP_K * tm,), lambda i: (i,), memory_space=pltpu.SMEM),
            pl.BlockSpec((TOP_K * tm,), lambda i: (jnp.minimum(i + 1, n_steps - 1),),
                         memory_space=pltpu.SMEM),
            pl.BlockSpec((tm, d), lambda i: (i, 0)),
            pl.BlockSpec((tm, LANES), lambda i: (i, 0)),
            pl.BlockSpec((1, d), lambda i: (0, 0)),
            pl.BlockSpec(memory_space=pl.ANY),
        ],
        out_specs=pl.BlockSpec((tm, d), lambda i: (i, 0)),
        scratch_shapes=[pltpu.VMEM((2, TOP_K, tm * SUBLANES, LANES), U32),
                        pltpu.SemaphoreType.DMA((2,))],
        compiler_params=pltpu.CompilerParams(
            dimension_semantics=("arbitrary",), vmem_limit_bytes=VMEM_LIMIT),
        name="combine_norm",
    )(dest_flat, dest_flat, x1, route_w, nrm, y_rows)


def _strict_lower_tri(n, dtype):
    r = lax.broadcasted_iota(I32, (n, n), 0)
    c = lax.broadcasted_iota(I32, (n, n), 1)
    return (r > c).astype(dtype)


def _pick_tile(n, pref):
    tile = pref
    while n % tile:
        tile //= 2
    return tile


def kernel(x, norm_mix, w_in, sinks, w_up_sb, w_up_sw, w_out, norm_ffn, w_router_group,
           b_router_group, w_router_expert, b_router_expert, w_gate, w_up, w_down, norm_final):
    batch, seq_len, d = x.shape
    t = batch * seq_len
    depth = norm_mix.shape[0]
    assert depth == 1, "the final RMSNorm is fused into the only layer's combine stage"
    assert d == 2048 and seq_len % BLOCK == 0 and t % 256 == 0

    tri = _strict_lower_tri(SB_WINDOW_BLOCKS * BLOCK, BF16)
    tri2 = jnp.concatenate([tri, tri], axis=0)
    rank_chunk = _pick_tile(t, 1024)
    ltri = _strict_lower_tri(rank_chunk, BF16)

    n_assign = t * TOP_K
    n_blocks = -(-n_assign // ROWS_PER_BLOCK) + N_EXPERTS
    n_rows = n_blocks * ROWS_PER_BLOCK

    x2d = x.reshape(t, d)
    for layer in range(depth):
        wl = w_in[layer]
        o = 0
        segs = {}
        for name, width in (("q_sb", SB_WIDTH), ("k_sb", SB_WIDTH), ("v_sb", SB_WIDTH),
                            ("q_sw", SW_WIDTH), ("k_sw", SW_KV_WIDTH), ("v_sw", SW_KV_WIDTH),
                            ("g_sb", d), ("g_sw", d)):
            segs[name] = wl[:, o:o + width]
            o += width
        w_perm = jnp.concatenate(
            [segs[n] for n in ("g_sb", "g_sw", "q_sw", "q_sb", "k_sb", "v_sb", "k_sw", "v_sw")],
            axis=1).astype(BF16)

        proj = _inproj(x2d, norm_mix[layer][None, :], w_perm,
                       tm=_pick_tile(t, 1024), tn=768, n_gate_cols=2 * d)
        o_sb = _sb_attention(proj, tri2, batch=batch, seq_len=seq_len)
        o_sw = _sw_attention(proj, sinks[layer], batch=batch, seq_len=seq_len)

        w_r = jnp.concatenate([w_router_group[layer], w_router_expert[layer]], axis=1)
        w_r = jnp.pad(w_r, ((0, 0), (0, LANES - w_r.shape[1])))
        wr_hi = w_r.astype(BF16)
        wr_lo = (w_r - wr_hi.astype(F32)).astype(BF16)
        b_r = jnp.concatenate([b_router_group[layer], b_router_expert[layer]])
        b_r = jnp.pad(b_r, (0, LANES - b_r.shape[0]))[None, :]

        x1, h2p, route_i, route_w = _mix_route(
            x2d, proj, o_sb, o_sw, w_up_sb[layer].astype(BF16), w_up_sw[layer].astype(BF16),
            w_out[layer].astype(BF16), norm_ffn[layer][None, :], wr_hi, wr_lo, b_r, tm=256)

        rank, counts = _ranks(route_i, ltri, chunk=rank_chunk)

        counts = counts[0, :N_EXPERTS]
        padded = (counts + ROWS_PER_BLOCK - 1) // ROWS_PER_BLOCK * ROWS_PER_BLOCK
        pend = jnp.cumsum(padded)
        pstart = pend - padded
        experts = jnp.arange(N_EXPERTS, dtype=I32)
        first_row = jnp.sum(
            jnp.where(route_i[:, :TOP_K, None] == experts, pstart, 0), axis=-1)
        dest = ((first_row + rank[:, :TOP_K]) * SUBLANES).reshape(-1).astype(I32)
        block_row0 = jnp.arange(n_blocks, dtype=I32) * ROWS_PER_BLOCK
        block_e = jnp.minimum(
            jnp.sum((pend[None, :] <= block_row0[:, None]).astype(I32), axis=1), N_EXPERTS - 1)
        n_used = (pend[-1] // ROWS_PER_BLOCK).astype(I32)[None]

        x_rows = _dispatch(((pstart + counts) * SUBLANES).astype(I32),
                           (padded - counts).astype(I32), n_used, dest, h2p, n_rows,
                           tm=_pick_tile(t, 1024))
        y_rows = _expert_ffn(block_e, n_used, x_rows, w_gate[layer], w_up[layer], w_down[layer])
        x2d = _combine(dest, x1, route_w, norm_final[None, :], y_rows, tm=256)
    return x2d.reshape(batch, seq_len, d)
```
